```python
import jax, jax.numpy as jnp
from jax import lax
import numpy as np

D_MODEL = 2048
BATCH = 1
SEQ = 8192
DEPTH = 1
DEC_BATCH = 16
DEC_SEQ = 64
PAST_LEN = 4096

CHUNK = 64
H_RET = 8
DK_RET = 128
DV_RET = 128
H_GDN = 8
DK_GDN = 128
DV_GDN = 128
CONV_W = 4
D_FF = 5632
ROPE_BASE = 10000.0
EPS = 1e-6

RET_QK = H_RET * DK_RET
RET_V = H_RET * DV_RET
GDN_QK = H_GDN * DK_GDN
GDN_V = H_GDN * DV_GDN
GDN_CONV_CH = 2 * GDN_QK + GDN_V
D_MIX = RET_V + GDN_V
IN_SIZES = [RET_QK, RET_QK, RET_V, RET_V, GDN_CONV_CH, GDN_V, H_GDN, H_GDN]
IN_SPLITS = [int(s) for s in np.cumsum(IN_SIZES)[:-1]]
N_IN = int(sum(IN_SIZES))

kernel_name = "hybrid_retention_gdn_macaron_step"


def _rmsnorm(x, g):
    xf = x.astype(jnp.float32)
    y = xf * lax.rsqrt(jnp.mean(xf * xf, axis=-1, keepdims=True) + EPS)
    return (y * g.astype(jnp.float32)).astype(x.dtype)


def _head_rmsnorm(o, g):
    return o * lax.rsqrt(jnp.mean(o * o, axis=-1, keepdims=True) + EPS) * g.astype(jnp.float32)


def _l2norm(x):
    return x * lax.rsqrt(jnp.sum(x * x, axis=-1, keepdims=True) + EPS)


def _swiglu(h, w_gate, w_up, w_down):
    return (jax.nn.silu(h @ w_gate) * (h @ w_up)) @ w_down


def _rotary(x, pos):
    half = x.shape[-1] // 2
    inv = ROPE_BASE ** (-jnp.arange(half, dtype=jnp.float32) / half)
    ang = pos[:, None] * inv[None, :]
    cos = jnp.cos(ang)[None, :, None, :]
    sin = jnp.sin(ang)[None, :, None, :]
    x1, x2 = x[..., :half], x[..., half:]
    return jnp.concatenate([x1 * cos - x2 * sin, x1 * sin + x2 * cos], axis=-1)


def _retention(q, k, v, s0, log_gamma):
    B, L, H, DK = q.shape
    DV = v.shape[-1]
    c = min(CHUNK, L)
    n = L // c
    qc = q.reshape(B, n, c, H, DK)
    kc = k.reshape(B, n, c, H, DK)
    vc = v.reshape(B, n, c, H, DV)
    t = jnp.arange(c, dtype=jnp.float32)
    diff = t[:, None] - t[None, :]
    causal = diff >= 0
    dmat = jnp.where(causal[None], jnp.exp(jnp.where(causal, diff, 0.0)[None] * log_gamma[:, None, None]), 0.0)
    scores = jnp.einsum('bnthd,bnshd->bnhts', qc, kc) * dmat
    o_intra = jnp.einsum('bnhts,bnshe->bnthe', scores, vc)
    q_dec = jnp.exp((t + 1.0)[:, None] * log_gamma[None, :])
    k_dec = jnp.exp((c - 1.0 - t)[:, None] * log_gamma[None, :])
    chunk_dec = jnp.exp(c * log_gamma)
    kv = jnp.einsum('bnshd,bnshe->bnhde', kc * k_dec[None, None, :, :, None], vc)

    def step(s, kv_i):
        return s * chunk_dec[None, :, None, None] + kv_i, s

    s_final, s_prev = lax.scan(step, s0, jnp.moveaxis(kv, 1, 0))
    s_prev = jnp.moveaxis(s_prev, 0, 1)
    o_inter = jnp.einsum('bnthd,bnhde->bnthe', qc * q_dec[None, None, :, :, None], s_prev)
    return (o_intra + o_inter).reshape(B, L, H, DV), s_final


def _gated_delta(q, k, v, g, beta, s0):
    B, L, H, DK = q.shape
    DV = v.shape[-1]
    c = min(CHUNK, L)
    n = L // c
    qc = q.reshape(B, n, c, H, DK).transpose(0, 1, 3, 2, 4)
    kc = k.reshape(B, n, c, H, DK).transpose(0, 1, 3, 2, 4)
    vc = v.reshape(B, n, c, H, DV).transpose(0, 1, 3, 2, 4)
    gc = jnp.cumsum(g.reshape(B, n, c, H).transpose(0, 1, 3, 2), axis=-1)
    bc = beta.reshape(B, n, c, H).transpose(0, 1, 3, 2)
    t = jnp.arange(c)
    tri = t[:, None] >= t[None, :]
    strict = t[:, None] > t[None, :]
    gdiff = gc[..., :, None] - gc[..., None, :]
    decay = jnp.where(tri, jnp.exp(jnp.where(tri, gdiff, 0.0)), 0.0)
    kk = jnp.einsum('bnhtd,bnhsd->bnhts', kc, kc)
    a_mat = jnp.where(strict, kk * decay * bc[..., :, None], 0.0)
    lhs = a_mat + jnp.eye(c, dtype=jnp.float32)
    w = lax.linalg.triangular_solve(lhs, kc * (bc * jnp.exp(gc))[..., None], left_side=True, lower=True, unit_diagonal=True)
    u = lax.linalg.triangular_solve(lhs, vc * bc[..., None], left_side=True, lower=True, unit_diagonal=True)
    qk = jnp.einsum('bnhtd,bnhsd->bnhts', qc, kc) * decay
    q_dec = qc * jnp.exp(gc)[..., None]
    k_dec = kc * jnp.exp(gc[..., -1:] - gc)[..., None]
    chunk_dec = jnp.exp(gc[..., -1])

    def step(s, inp):
        w_i, u_i, qk_i, q_i, kd_i, cd_i = inp
        delta = u_i - jnp.einsum('bhtd,bhde->bhte', w_i, s)
        o = jnp.einsum('bhtd,bhde->bhte', q_i, s) + jnp.einsum('bhts,bhse->bhte', qk_i, delta)
        s_new = s * cd_i[..., None, None] + jnp.einsum('bhtd,bhte->bhde', kd_i, delta)
        return s_new, o

    xs = tuple(jnp.moveaxis(a, 1, 0) for a in (w, u, qk, q_dec, k_dec, chunk_dec))
    s_final, o = lax.scan(step, s0, xs)
    o = o.transpose(1, 0, 3, 2, 4).reshape(B, L, H, DV)
    return o, s_final


def _mixer(h, s_ret, s_gdn, conv_buf, pos, w_in, ret_norm, conv_w, a_log, dt_bias, gdn_norm, w_out):
    B, L, _ = h.shape
    f32 = jnp.float32
    p = (h @ w_in).astype(f32)
    rq, rk, rv, rg, gqkv, gg, ga, gb = jnp.split(p, IN_SPLITS, axis=-1)
    log_gamma = jnp.log(1.0 - 2.0 ** (-5.0 - jnp.arange(H_RET, dtype=f32)))
    rq = _rotary(rq.reshape(B, L, H_RET, DK_RET), pos) * (DK_RET ** -0.5)
    rk = _rotary(rk.reshape(B, L, H_RET, DK_RET), pos)
    o_r, s_ret_new = _retention(rq, rk, rv.reshape(B, L, H_RET, DV_RET), s_ret.astype(f32), log_gamma)
    o_r = _head_rmsnorm(o_r, ret_norm.reshape(H_RET, DV_RET)) * jax.nn.silu(rg).reshape(B, L, H_RET, DV_RET)
    xpad = jnp.concatenate([conv_buf.astype(f32), gqkv], axis=1)
    conv_new = xpad[:, L:]
    cw = conv_w.astype(f32)
    gqkv = jax.nn.silu(sum(xpad[:, i:i + L] * cw[i] for i in range(CONV_W)))
    gq, gk, gv = jnp.split(gqkv, [GDN_QK, 2 * GDN_QK], axis=-1)
    gq = _l2norm(gq.reshape(B, L, H_GDN, DK_GDN)) * (DK_GDN ** -0.5)
    gk = _l2norm(gk.reshape(B, L, H_GDN, DK_GDN))
    gv = gv.reshape(B, L, H_GDN, DV_GDN)
    g_log = -jnp.exp(a_log.astype(f32)) * jax.nn.softplus(ga + dt_bias.astype(f32))
    beta = jax.nn.sigmoid(gb)
    o_g, s_gdn_new = _gated_delta(gq, gk, gv, g_log, beta, s_gdn.astype(f32))
    o_g = _head_rmsnorm(o_g, gdn_norm) * jax.nn.silu(gg).reshape(B, L, H_GDN, DV_GDN)
    o = jnp.concatenate([o_r.reshape(B, L, RET_V), o_g.reshape(B, L, GDN_V)], axis=-1).astype(h.dtype)
    return o @ w_out, s_ret_new.astype(s_ret.dtype), s_gdn_new.astype(s_gdn.dtype), conv_new.astype(conv_buf.dtype)


def _layer(x, s_ret, s_gdn, conv_buf, pos, lp):
    (ffn1_norm, ffn1_w_gate, ffn1_w_up, ffn1_w_down, mix_norm, w_in, ret_norm, gdn_conv,
     gdn_a_log, gdn_dt_bias, gdn_norm, w_out, ffn2_norm, ffn2_w_gate, ffn2_w_up, ffn2_w_down) = lp
    x = x + 0.5 * _swiglu(_rmsnorm(x, ffn1_norm), ffn1_w_gate, ffn1_w_up, ffn1_w_down)
    m, s_ret, s_gdn, conv_buf = _mixer(_rmsnorm(x, mix_norm), s_ret, s_gdn, conv_buf, pos, w_in, ret_norm,
                                       gdn_conv, gdn_a_log, gdn_dt_bias, gdn_norm, w_out)
    x = x + m
    x = x + 0.5 * _swiglu(_rmsnorm(x, ffn2_norm), ffn2_w_gate, ffn2_w_up, ffn2_w_down)
    return x, s_ret, s_gdn, conv_buf


def setup_inputs(seed: int = 0) -> dict:
    key = jax.random.key(seed)
    ks = jax.random.split(key, 24)
    f32 = jnp.float32

    def nrm(k, shape, scale):
        return jax.random.normal(k, shape, f32) * scale

    def gain(k, shape):
        return 1.0 + 0.02 * jax.random.normal(k, shape, f32)

    dt = jnp.exp(jax.random.uniform(ks[14], (DEPTH, H_GDN), f32, np.log(1e-3), np.log(1e-1)))
    return {
        "x_prompt": nrm(ks[0], (BATCH, SEQ, D_MODEL), 1.0),
        "x_sample": nrm(ks[1], (DEC_BATCH, DEC_SEQ, D_MODEL), 1.0),
        "state_ret": nrm(ks[2], (DEPTH, DEC_BATCH, H_RET, DK_RET, DV_RET), 0.1),
        "state_gdn": nrm(ks[3], (DEPTH, DEC_BATCH, H_GDN, DK_GDN, DV_GDN), 0.1),
        "state_conv": nrm(ks[4], (DEPTH, DEC_BATCH, CONV_W - 1, GDN_CONV_CH), 1.0),
        "ffn1_norm": gain(ks[5], (DEPTH, D_MODEL)),
        "ffn1_w_gate": nrm(ks[6], (DEPTH, D_MODEL, D_FF), D_MODEL ** -0.5),
        "ffn1_w_up": nrm(ks[7], (DEPTH, D_MODEL, D_FF), D_MODEL ** -0.5),
        "ffn1_w_down": nrm(ks[8], (DEPTH, D_FF, D_MODEL), D_FF ** -0.5),
        "mix_norm": gain(ks[9], (DEPTH, D_MODEL)),
        "w_in": nrm(ks[10], (DEPTH, D_MODEL, N_IN), D_MODEL ** -0.5),
        "ret_norm": gain(ks[11], (DEPTH, RET_V)),
        "gdn_conv": nrm(ks[12], (DEPTH, CONV_W, GDN_CONV_CH), CONV_W ** -0.5),
        "gdn_a_log": jnp.log(jax.random.uniform(ks[13], (DEPTH, H_GDN), f32, 1.0, 16.0)),
        "gdn_dt_bias": dt + jnp.log(-jnp.expm1(-dt)),
        "gdn_norm": gain(ks[15], (DEPTH, DV_GDN)),
        "w_out": nrm(ks[16], (DEPTH, D_MIX, D_MODEL), D_MIX ** -0.5),
        "ffn2_norm": gain(ks[17], (DEPTH, D_MODEL)),
        "ffn2_w_gate": nrm(ks[18], (DEPTH, D_MODEL, D_FF), D_MODEL ** -0.5),
        "ffn2_w_up": nrm(ks[19], (DEPTH, D_MODEL, D_FF), D_MODEL ** -0.5),
        "ffn2_w_down": nrm(ks[20], (DEPTH, D_FF, D_MODEL), D_FF ** -0.5),
        "final_norm": gain(ks[21], (D_MODEL,)),
    }


def reference(x_prompt, x_sample, state_ret, state_gdn, state_conv, ffn1_norm, ffn1_w_gate, ffn1_w_up,
              ffn1_w_down, mix_norm, w_in, ret_norm, gdn_conv, gdn_a_log, gdn_dt_bias, gdn_norm, w_out,
              ffn2_norm, ffn2_w_gate, ffn2_w_up, ffn2_w_down, final_norm):
    Bp, Lp, _ = x_prompt.shape
    Ls = x_sample.shape[1]
    pos_p = jnp.arange(Lp, dtype=jnp.float32)
    pos_s = PAST_LEN + jnp.arange(Ls, dtype=jnp.float32)
    yp, ys = x_prompt, x_sample
    rp, gp, cp, rs, gs, cs = [], [], [], [], [], []
    for l in range(DEPTH):
        lp = (ffn1_norm[l], ffn1_w_gate[l], ffn1_w_up[l], ffn1_w_down[l], mix_norm[l], w_in[l], ret_norm[l],
              gdn_conv[l], gdn_a_log[l], gdn_dt_bias[l], gdn_norm[l], w_out[l], ffn2_norm[l], ffn2_w_gate[l],
              ffn2_w_up[l], ffn2_w_down[l])
        z_ret = jnp.zeros((Bp, H_RET, DK_RET, DV_RET), state_ret.dtype)
        z_gdn = jnp.zeros((Bp, H_GDN, DK_GDN, DV_GDN), state_gdn.dtype)
        z_conv = jnp.zeros((Bp, CONV_W - 1, GDN_CONV_CH), state_conv.dtype)
        yp, a, b, c = _layer(yp, z_ret, z_gdn, z_conv, pos_p, lp)
        ys, d, e, f = _layer(ys, state_ret[l], state_gdn[l], state_conv[l], pos_s, lp)
        rp.append(a); gp.append(b); cp.append(c)
        rs.append(d); gs.append(e); cs.append(f)
    y_prompt = _rmsnorm(yp, final_norm)
    y_sample = _rmsnorm(ys, final_norm)
    return (y_prompt, y_sample, jnp.stack(rp), jnp.stack(gp), jnp.stack(cp), jnp.stack(rs), jnp.stack(gs), jnp.stack(cs))
```

```python
import functools
import math

import jax
import jax.numpy as jnp
import numpy as np
from jax import lax
from jax.experimental import pallas as pl
from jax.experimental.pallas import tpu as pltpu

F32 = jnp.float32
BF16 = jnp.bfloat16

D_MODEL = 2048
D_FF = 5632
N_HEADS = 8
D_HEAD = 128
CONV_W = 4
PAST_LEN = 4096
ROPE_BASE = 10000.0
EPS = 1e-6

LANES = 128
SUBLANES = 8

COL_RQ, COL_RK, COL_RV, COL_RG = 0, 8, 16, 24
COL_GQ, COL_GK, COL_GV, COL_GG = 32, 40, 48, 56
COL_AB = 64
N_MAIN = 64 * LANES
N_PAD = 65 * LANES

VMEM_LIMIT = 56 * 1024 * 1024


def _bdot(a, b):
    return jnp.dot(a.astype(BF16), b.astype(BF16), preferred_element_type=F32)


def _bdot_nt(a, b):
    return lax.dot_general(a.astype(BF16), b.astype(BF16), (((1,), (1,)), ((), ())),
                           preferred_element_type=F32)


def _bdot_tn(a, b):
    return lax.dot_general(a.astype(BF16), b.astype(BF16), (((0,), (0,)), ((), ())),
                           preferred_element_type=F32)


def _rms(x, g):
    return x * lax.rsqrt(jnp.mean(x * x, axis=-1, keepdims=True) + EPS) * g


def _silu(x):
    return x * jax.nn.sigmoid(x)


def _ffn_kernel(x_ref, g_ref, wg_ref, wu_ref, wd_ref, fin_ref, o_ref, h_ref, *, final_norm):
    j = pl.program_id(1)

    @pl.when(j == 0)
    def _():
        x = x_ref[...]
        h_ref[...] = _rms(x, g_ref[...]).astype(BF16)
        o_ref[...] = x

    h = h_ref[...]
    a = jnp.dot(h, wg_ref[...], preferred_element_type=F32)
    b = jnp.dot(h, wu_ref[...], preferred_element_type=F32)
    z = (0.5 * _silu(a) * b).astype(BF16)
    o_ref[...] += jnp.dot(z, wd_ref[...], preferred_element_type=F32)

    if final_norm:
        @pl.when(j == pl.num_programs(1) - 1)
        def _():
            o_ref[...] = _rms(o_ref[...], fin_ref[...])


def _ffn(x, g, wg, wu, wd, fin, *, final_norm, tm=512, tf=512):
    t = x.shape[0]
    grid = (t // tm, D_FF // tf)
    return pl.pallas_call(
        functools.partial(_ffn_kernel, final_norm=final_norm),
        grid=grid,
        in_specs=[
            pl.BlockSpec((tm, D_MODEL), lambda i, j: (i, 0)),
            pl.BlockSpec((1, D_MODEL), lambda i, j: (0, 0)),
            pl.BlockSpec((D_MODEL, tf), lambda i, j: (0, j)),
            pl.BlockSpec((D_MODEL, tf), lambda i, j: (0, j)),
            pl.BlockSpec((tf, D_MODEL), lambda i, j: (j, 0)),
            pl.BlockSpec((1, D_MODEL), lambda i, j: (0, 0)),
        ],
        out_specs=pl.BlockSpec((tm, D_MODEL), lambda i, j: (i, 0)),
        out_shape=jax.ShapeDtypeStruct((t, D_MODEL), F32),
        scratch_shapes=[pltpu.VMEM((tm, D_MODEL), BF16)],
        compiler_params=pltpu.CompilerParams(
            dimension_semantics=("parallel", "arbitrary"), vmem_limit_bytes=VMEM_LIMIT),
        name="ffn_final" if final_norm else "ffn",
    )(x, g, wg, wu, wd, fin)


def _in_proj_kernel(x_ref, g_ref, w_ref, o_ref, h_ref):
    @pl.when(pl.program_id(1) == 0)
    def _():
        h_ref[...] = _rms(x_ref[...], g_ref[...]).astype(BF16)

    o_ref[...] = jnp.dot(h_ref[...], w_ref[...], preferred_element_type=F32)


def _in_proj(x, g, w, *, tm=512, tn=13 * LANES):
    t = x.shape[0]
    n = w.shape[1]
    return pl.pallas_call(
        _in_proj_kernel,
        grid=(t // tm, n // tn),
        in_specs=[
            pl.BlockSpec((tm, D_MODEL), lambda i, j: (i, 0)),
            pl.BlockSpec((1, D_MODEL), lambda i, j: (0, 0)),
            pl.BlockSpec((D_MODEL, tn), lambda i, j: (0, j)),
        ],
        out_specs=pl.BlockSpec((tm, tn), lambda i, j: (i, j)),
        out_shape=jax.ShapeDtypeStruct((t, n), F32),
        scratch_shapes=[pltpu.VMEM((tm, D_MODEL), BF16)],
        compiler_params=pltpu.CompilerParams(
            dimension_semantics=("parallel", "arbitrary"), vmem_limit_bytes=VMEM_LIMIT),
        name="in_proj",
    )(x, g, w)


def _out_proj_kernel(x_ref, o_ref, w_ref, y_ref):
    y_ref[...] = x_ref[...] + jnp.dot(o_ref[...], w_ref[...], preferred_element_type=F32)


def _out_proj(x, o, w, *, tm=512):
    t = x.shape[0]
    return pl.pallas_call(
        _out_proj_kernel,
        grid=(t // tm,),
        in_specs=[
            pl.BlockSpec((tm, D_MODEL), lambda i: (i, 0)),
            pl.BlockSpec((tm, D_MODEL), lambda i: (i, 0)),
            pl.BlockSpec((D_MODEL, D_MODEL), lambda i: (0, 0)),
        ],
        out_specs=pl.BlockSpec((tm, D_MODEL), lambda i: (i, 0)),
        out_shape=jax.ShapeDtypeStruct((t, D_MODEL), F32),
        compiler_params=pltpu.CompilerParams(
            dimension_semantics=("parallel",), vmem_limit_bytes=VMEM_LIMIT),
        name="out_proj",
    )(x, o, w)


def _rope_kernel(inv_ref, cos_ref, sin_ref, *, rows):
    pos = (lax.broadcasted_iota(jnp.int32, (rows, LANES), 0) + pl.program_id(0) * rows).astype(F32)
    lane = lax.broadcasted_iota(jnp.int32, (rows, LANES), 1)
    ang = pos * inv_ref[...]
    cos_ref[...] = jnp.cos(ang)
    s = jnp.sin(ang)
    sin_ref[...] = jnp.where(lane < LANES // 2, -s, s)


def _rope_tables(inv2, n_pos, *, rows=512):
    return pl.pallas_call(
        functools.partial(_rope_kernel, rows=rows),
        grid=(n_pos // rows,),
        in_specs=[pl.BlockSpec((1, LANES), lambda i: (0, 0))],
        out_specs=[pl.BlockSpec((rows, LANES), lambda i: (i, 0))] * 2,
        out_shape=[jax.ShapeDtypeStruct((n_pos, LANES), F32)] * 2,
        compiler_params=pltpu.CompilerParams(dimension_semantics=("parallel",)),
        name="rope_tables",
    )(inv2)


def _conv_silu(cur_ref, prev_ref, cw_ref, zero_prev):
    x = cur_ref[...]
    pv = prev_ref[...]
    if zero_prev is not None:
        pv = jnp.where(zero_prev, 0.0, pv)
    cw = cw_ref[...]
    row8 = lax.broadcasted_iota(jnp.int32, (SUBLANES, LANES), 0)
    acc = x * cw[CONV_W - 1:CONV_W, :]
    for j in range(1, CONV_W):
        y = pltpu.roll(x, j, axis=0)
        top = jnp.where(row8 < j, pltpu.roll(pv, j, axis=0), y[:SUBLANES])
        y = jnp.concatenate([top, y[SUBLANES:]], axis=0)
        acc = acc + y * cw[CONV_W - 1 - j:CONV_W - j, :]
    return _silu(acc)


def _lane_pick(x, idx):
    lane = lax.broadcasted_iota(jnp.int32, x.shape, 1)
    return jnp.sum(jnp.where(lane == idx, x, 0.0), axis=1, keepdims=True)


def _mixer_kernel(rq_ref, rk_ref, rv_ref, rg_ref, gq_ref, gk_ref, gv_ref, gg_ref,
                  pq_ref, pk_ref, pv_ref, ab_ref, cos_ref, sin_ref,
                  cwq_ref, cwk_ref, cwv_ref, alog_ref, dtb_ref, lgam_ref, rnorm_ref, gnorm_ref,
                  sr0_ref, sg0_ref,
                  or_ref, og_ref, sr_ref, sg_ref,
                  sr_scr, sg_scr, *, chunk, carry):
    c = chunk
    h = pl.program_id(0)
    i = pl.program_id(1)
    scale = D_HEAD ** -0.5

    if carry:
        @pl.when(i == 0)
        def _():
            sr_scr[...] = jnp.zeros_like(sr_scr)
            sg_scr[...] = jnp.zeros_like(sg_scr)
        s_ret = sr_scr[...]
        s_gdn = sg_scr[...]
        zero_prev = i == 0
    else:
        s_ret = sr0_ref[...]
        s_gdn = sg0_ref[...]
        zero_prev = None

    row = lax.broadcasted_iota(jnp.int32, (c, c), 0)
    col = lax.broadcasted_iota(jnp.int32, (c, c), 1)
    tri = row >= col
    tcol = lax.broadcasted_iota(jnp.int32, (c, 1), 0).astype(F32)

    lg = lgam_ref[...][:, :1]
    cosv = cos_ref[...]
    sinv = sin_ref[...]
    q = rq_ref[...]
    k = rk_ref[...]
    q = (q * cosv + pltpu.roll(q, LANES // 2, axis=1) * sinv) * scale
    k = k * cosv + pltpu.roll(k, LANES // 2, axis=1) * sinv
    v = rv_ref[...]
    dmat = jnp.where(tri, jnp.exp(jnp.where(tri, row - col, 0).astype(F32) * lg), 0.0)
    scores = _bdot_nt(q, k) * dmat
    o = _bdot(scores, v) + _bdot(q * jnp.exp((tcol + 1.0) * lg), s_ret)
    s_ret_new = s_ret * jnp.exp(float(c) * lg) + _bdot_tn(k * jnp.exp((float(c) - 1.0 - tcol) * lg), v)
    o = o * lax.rsqrt(jnp.mean(o * o, axis=-1, keepdims=True) + EPS) * rnorm_ref[...]
    or_ref[...] = (o * _silu(rg_ref[...])).astype(or_ref.dtype)

    q = _conv_silu(gq_ref, pq_ref, cwq_ref, zero_prev)
    k = _conv_silu(gk_ref, pk_ref, cwk_ref, zero_prev)
    v = _conv_silu(gv_ref, pv_ref, cwv_ref, zero_prev)
    q = q * lax.rsqrt(jnp.sum(q * q, axis=-1, keepdims=True) + EPS) * scale
    k = k * lax.rsqrt(jnp.sum(k * k, axis=-1, keepdims=True) + EPS)

    ab = ab_ref[...]
    g_all = -jnp.exp(alog_ref[...]) * jax.nn.softplus(ab + dtb_ref[...])
    beta = _lane_pick(jax.nn.sigmoid(ab), N_HEADS + h)
    g1 = g_all.astype(BF16)
    r1 = g_all - g1.astype(F32)
    g2 = r1.astype(BF16)
    g3 = (r1 - g2.astype(F32)).astype(BF16)
    trib = tri.astype(BF16)
    gc_all = (jnp.dot(trib, g1, preferred_element_type=F32)
              + jnp.dot(trib, g2, preferred_element_type=F32)
              + jnp.dot(trib, g3, preferred_element_type=F32))
    gc = _lane_pick(gc_all, h)
    gc_row = jnp.sum(jnp.where(row == col, gc, 0.0), axis=0, keepdims=True)
    gc_last = gc_row[:, c - 1:c]

    decay = jnp.where(tri, jnp.exp(jnp.where(tri, gc - gc_row, 0.0)), 0.0)
    qkk = _bdot_nt(jnp.concatenate([q, k], axis=0), k)
    qk = qkk[:c] * decay
    a_mat = jnp.where(row > col, qkk[c:] * decay * beta, 0.0)

    t_inv = jnp.where(row == col, 1.0, 0.0) - jnp.where(((row & 1) == 1) & (col == row - 1), a_mat, 0.0)
    b = 2
    while b < c:
        sh = int(math.log2(b))
        rb = row >> sh
        cb = col >> sh
        a_off = jnp.where(((rb & 1) == 1) & (cb == rb - 1), a_mat, 0.0)
        t_inv = t_inv - _bdot(t_inv, _bdot(a_off, t_inv))
        b *= 2

    e_gc = jnp.exp(gc)
    wu = _bdot(t_inv, jnp.concatenate([k * (beta * e_gc), v * beta], axis=1))
    w = wu[:, :D_HEAD]
    u = wu[:, D_HEAD:]
    wq = _bdot(jnp.concatenate([w, q * e_gc], axis=0), s_gdn)
    delta = u - wq[:c]
    o = wq[c:] + _bdot(qk, delta)
    s_gdn_new = s_gdn * jnp.exp(gc_last) + _bdot_tn(k * jnp.exp(gc_last - gc), delta)
    o = o * lax.rsqrt(jnp.mean(o * o, axis=-1, keepdims=True) + EPS) * gnorm_ref[...]
    og_ref[...] = (o * _silu(gg_ref[...])).astype(og_ref.dtype)

    if carry:
        sr_scr[...] = s_ret_new
        sg_scr[...] = s_gdn_new
    sr_ref[...] = s_ret_new
    sg_ref[...] = s_gdn_new


def _mixer(p, prev, cos2, sin2, conv_w, alog, dtb, lgam, rnorm, gnorm, s_ret0, s_gdn0, *,
           chunk, carry, pos_block0):
    t = p.shape[0]
    n = t // chunk

    def pcol(cb):
        return pl.BlockSpec((chunk, LANES), lambda h, i, cb=cb: (i, cb + h))

    if carry:
        rpb = chunk // SUBLANES

        def prevcol(cb):
            return pl.BlockSpec((SUBLANES, LANES),
                                lambda h, i, cb=cb: (jnp.maximum(i * rpb - 1, 0), cb + h))
        prev_arr = p
        state_in_spec = pl.BlockSpec((None, D_HEAD, D_HEAD), lambda h, i: (h, 0, 0))
        state_out_spec = pl.BlockSpec((None, D_HEAD, D_HEAD), lambda h, i: (h, 0, 0))
        state_shape = jax.ShapeDtypeStruct((N_HEADS, D_HEAD, D_HEAD), F32)
    else:
        def prevcol(cb):
            return pl.BlockSpec((None, SUBLANES, LANES), lambda h, i, cb=cb: (i, 0, cb - COL_GQ + h))
        prev_arr = prev
        state_in_spec = pl.BlockSpec((None, None, D_HEAD, D_HEAD), lambda h, i: (i, h, 0, 0))
        state_out_spec = pl.BlockSpec((None, None, D_HEAD, D_HEAD), lambda h, i: (i, h, 0, 0))
        state_shape = jax.ShapeDtypeStruct((n, N_HEADS, D_HEAD, D_HEAD), F32)

    if carry:
        pos_spec = pl.BlockSpec((chunk, LANES), lambda h, i: (i, 0))
    else:
        pos_spec = pl.BlockSpec((chunk, LANES), lambda h, i: (pos_block0, 0))

    def cwcol(cb):
        return pl.BlockSpec((CONV_W, LANES), lambda h, i, cb=cb: (0, cb + h))

    vec = pl.BlockSpec((1, LANES), lambda h, i: (0, 0))
    in_specs = [
        pcol(COL_RQ), pcol(COL_RK), pcol(COL_RV), pcol(COL_RG),
        pcol(COL_GQ), pcol(COL_GK), pcol(COL_GV), pcol(COL_GG),
        prevcol(COL_GQ), prevcol(COL_GK), prevcol(COL_GV),
        pl.BlockSpec((chunk, LANES), lambda h, i: (i, COL_AB)),
        pos_spec, pos_spec,
        cwcol(0), cwcol(N_HEADS), cwcol(2 * N_HEADS),
        vec, vec,
        pl.BlockSpec((None, 1, LANES), lambda h, i: (h, 0, 0)),
        pl.BlockSpec((1, LANES), lambda h, i: (0, h)),
        vec,
        state_in_spec, state_in_spec,
    ]
    out_specs = [
        pl.BlockSpec((chunk, LANES), lambda h, i: (i, h)),
        pl.BlockSpec((chunk, LANES), lambda h, i: (i, h)),
        state_out_spec, state_out_spec,
    ]
    out_shape = [
        jax.ShapeDtypeStruct((t, N_HEADS * D_HEAD), BF16),
        jax.ShapeDtypeStruct((t, N_HEADS * D_HEAD), BF16),
        state_shape, state_shape,
    ]
    return pl.pallas_call(
        functools.partial(_mixer_kernel, chunk=chunk, carry=carry),
        grid=(N_HEADS, n),
        in_specs=in_specs,
        out_specs=out_specs,
        out_shape=out_shape,
        scratch_shapes=[pltpu.VMEM((D_HEAD, D_HEAD), F32), pltpu.VMEM((D_HEAD, D_HEAD), F32)],
        compiler_params=pltpu.CompilerParams(
            dimension_semantics=("arbitrary", "arbitrary"), vmem_limit_bytes=VMEM_LIMIT),
        name="mixer_carry" if carry else "mixer_streams",
    )(p, p, p, p, p, p, p, p, prev_arr, prev_arr, prev_arr, p, cos2, sin2,
      conv_w, conv_w, conv_w, alog, dtb, lgam, rnorm, gnorm, s_ret0, s_gdn0)


def _pad_lanes(v):
    return jnp.pad(v.astype(F32), (0, LANES - v.shape[0])).reshape(1, LANES)


def kernel(x_prompt, x_sample, state_ret, state_gdn, state_conv, ffn1_norm, ffn1_w_gate, ffn1_w_up,
           ffn1_w_down, mix_norm, w_in, ret_norm, gdn_conv, gdn_a_log, gdn_dt_bias, gdn_norm, w_out,
           ffn2_norm, ffn2_w_gate, ffn2_w_up, ffn2_w_down, final_norm):
    depth = w_in.shape[0]
    bp, lp, _ = x_prompt.shape
    bs, ls, _ = x_sample.shape
    assert depth == 1 and bp == 1, "kernel is specialised to one layer and one prompt stream"
    prompt_chunk = 256
    assert lp % prompt_chunk == 0 and ls == 64

    half = D_HEAD // 2
    inv = ROPE_BASE ** (-jnp.arange(half, dtype=F32) / half)
    inv2 = jnp.concatenate([inv, inv]).reshape(1, LANES)
    cos2, sin2 = _rope_tables(inv2, max(lp, PAST_LEN + ls))
    lgam = jnp.log(1.0 - 2.0 ** (-5.0 - jnp.arange(N_HEADS, dtype=F32)))
    lgam = jnp.broadcast_to(lgam[:, None, None], (N_HEADS, 1, LANES))

    l = 0
    fin = final_norm.reshape(1, D_MODEL)
    g1 = ffn1_norm[l].reshape(1, D_MODEL)
    g2 = ffn2_norm[l].reshape(1, D_MODEL)
    gm = mix_norm[l].reshape(1, D_MODEL)
    w1g, w1u, w1d = (w[l].astype(BF16) for w in (ffn1_w_gate, ffn1_w_up, ffn1_w_down))
    w2g, w2u, w2d = (w[l].astype(BF16) for w in (ffn2_w_gate, ffn2_w_up, ffn2_w_down))
    n_in = w_in.shape[2]
    wi = jnp.pad(w_in[l], ((0, 0), (0, N_PAD - n_in))).astype(BF16)
    wo = w_out[l].astype(BF16)
    alog = _pad_lanes(gdn_a_log[l])
    dtb = _pad_lanes(gdn_dt_bias[l])
    rnorm = ret_norm[l].reshape(1, N_HEADS * D_HEAD)
    gnorm = gdn_norm[l].reshape(1, D_HEAD)
    conv_w = gdn_conv[l]
    conv_ch = conv_w.shape[1]

    def layer(x, *, chunk, carry, prev, s_ret0, s_gdn0, pos_block0):
        x = _ffn(x, g1, w1g, w1u, w1d, fin, final_norm=False)
        p = _in_proj(x, gm, wi)
        o_r, o_g, s_r, s_g = _mixer(p, prev, cos2, sin2, conv_w, alog, dtb, lgam, rnorm, gnorm,
                                    s_ret0, s_gdn0, chunk=chunk, carry=carry, pos_block0=pos_block0)
        x = _out_proj(x, jnp.concatenate([o_r, o_g], axis=1), wo)
        x = _ffn(x, g2, w2g, w2u, w2d, fin, final_norm=True)
        return x, p, s_r, s_g

    zeros_state = jnp.zeros((N_HEADS, D_HEAD, D_HEAD), F32)
    yp, pp, rp, gp = layer(x_prompt.reshape(lp, D_MODEL), chunk=prompt_chunk, carry=True, prev=None,
                           s_ret0=zeros_state, s_gdn0=zeros_state, pos_block0=0)
    prev_s = jnp.pad(state_conv[l], ((0, 0), (SUBLANES - (CONV_W - 1), 0), (0, 0)))
    ys, ps, rs, gs = layer(x_sample.reshape(bs * ls, D_MODEL), chunk=ls, carry=False, prev=prev_s,
                           s_ret0=state_ret[l], s_gdn0=state_gdn[l], pos_block0=PAST_LEN // ls)

    c0 = COL_GQ * LANES
    conv_p = pp[lp - (CONV_W - 1):, c0:c0 + conv_ch].reshape(1, 1, CONV_W - 1, conv_ch)
    conv_s = ps.reshape(bs, ls, N_PAD)[:, ls - (CONV_W - 1):, c0:c0 + conv_ch].reshape(1, bs, CONV_W - 1, conv_ch)
    return (yp.reshape(bp, lp, D_MODEL), ys.reshape(bs, ls, D_MODEL),
            rp.reshape(1, 1, N_HEADS, D_HEAD, D_HEAD), gp.reshape(1, 1, N_HEADS, D_HEAD, D_HEAD), conv_p,
            rs.reshape(1, bs, N_HEADS, D_HEAD, D_HEAD), gs.reshape(1, bs, N_HEADS, D_HEAD, D_HEAD), conv_s)
```
